```python
import math
import jax, jax.numpy as jnp
from jax import lax
import numpy as np

D_MODEL = 1024
BATCH = 16
SEQ = 2048
DEPTH = 1
DEC_BATCH = 32
DEC_SEQ = 8
PAST_LEN = 16384
PAGE_SIZE = 128

ATT_WIDTH = D_MODEL // 2
SSM_WIDTH = D_MODEL - ATT_WIDTH
HEAD_DIM = 64
N_ATT_HEADS = ATT_WIDTH // HEAD_DIM
MOBA_BLOCK = 256
MOBA_TOPK = 3
N_SEL = MOBA_TOPK + 1
QUERY_CHUNK = 128
SSM_GROUP = 16
N_SSM_GROUPS = SSM_WIDTH // SSM_GROUP
SSM_STATE = 64
D_FF = ((8 * D_MODEL // 3 + 127) // 128) * 128
NUM_BUCKETS = 32
MAX_DISTANCE = 128
N_MOD = 9
RMS_EPS = 1e-6

kernel_name = "hymba_moba_s5_macaron_decode_step"


def _rms_norm(x, g):
    xf = x.astype(jnp.float32)
    y = xf * lax.rsqrt(jnp.mean(xf * xf, axis=-1, keepdims=True) + RMS_EPS)
    return (y * g.astype(jnp.float32)).astype(x.dtype)


def _swiglu(h, wg, wu, wd):
    return (jax.nn.silu(h @ wg) * (h @ wu)) @ wd


def _rel_bucket(dist):
    max_exact = NUM_BUCKETS // 2
    n = jnp.maximum(dist, 0)
    nf = jnp.maximum(n, 1).astype(jnp.float32)
    large = max_exact + (jnp.log(nf / max_exact) / math.log(MAX_DISTANCE / max_exact)
                         * (NUM_BUCKETS - max_exact)).astype(jnp.int32)
    return jnp.where(n < max_exact, n, jnp.minimum(large, NUM_BUCKETS - 1))


def _moba_attend(q, k_sel, v_sel, key_pos, q_pos, valid, rel_bias):
    nq, nh, dh = q.shape
    qp = q_pos[:, None, None, None]
    logits = jnp.einsum('qhd,qhnkd->qhnk', q, k_sel).astype(jnp.float32) * (dh ** -0.5)
    h_idx = jnp.arange(nh)[None, :, None, None]
    bias = rel_bias[_rel_bucket(qp - key_pos), h_idx].astype(jnp.float32)
    mask = valid[..., None] & (key_pos <= qp)
    logits = jnp.where(mask, logits + bias, -jnp.inf)
    probs = jax.nn.softmax(logits.reshape(nq, nh, -1), axis=-1).reshape(logits.shape)
    return jnp.einsum('qhnk,qhnkd->qhd', probs.astype(v_sel.dtype), v_sel)


def _select_blocks(q, means, own):
    bt, ln, nh, _ = q.shape
    n_cand = means.shape[1]
    scores = jnp.einsum('blhd,bnhd->blhn', q.astype(jnp.float32), means)
    ownb = own[None, :, None, None]
    scores = jnp.where(jnp.arange(n_cand)[None, None, None, :] < ownb, scores, -jnp.inf)
    if n_cand < MOBA_TOPK:
        scores = jnp.pad(scores, ((0, 0), (0, 0), (0, 0), (0, MOBA_TOPK - n_cand)),
                         constant_values=-jnp.inf)
    _, idx = lax.top_k(scores, MOBA_TOPK)
    idx = idx.astype(jnp.int32)
    own_full = jnp.broadcast_to(ownb, (bt, ln, nh, 1)).astype(jnp.int32)
    sel = jnp.concatenate([idx, own_full], axis=-1)
    valid = jnp.concatenate([idx < ownb, jnp.ones((bt, ln, nh, 1), bool)], axis=-1)
    return sel, valid


def _moba_prompt(q, k, v, rel_bias):
    bt, s, nh, dh = q.shape
    nb = -(-s // MOBA_BLOCK)
    pad = nb * MOBA_BLOCK - s

    def to_blocks(t):
        t = jnp.pad(t, ((0, 0), (0, pad), (0, 0), (0, 0)))
        return t.reshape(bt, nb, MOBA_BLOCK, nh, dh).transpose(0, 1, 3, 2, 4)

    kb, vb = to_blocks(k), to_blocks(v)
    means = jnp.mean(kb, axis=3, dtype=jnp.float32)
    q_pos = jnp.arange(s, dtype=jnp.int32)
    sel, valid = _select_blocks(q, means, q_pos // MOBA_BLOCK)
    n_chunks = s // QUERY_CHUNK
    h_idx = jnp.arange(nh)[None, :, None]
    offs = jnp.arange(MOBA_BLOCK, dtype=jnp.int32)

    def chunk(args):
        qc, selc, validc, qposc, b = args
        selg = jnp.minimum(selc, nb - 1)
        k_sel = kb[b, selg, h_idx]
        v_sel = vb[b, selg, h_idx]
        key_pos = selg[..., None] * MOBA_BLOCK + offs
        return _moba_attend(qc, k_sel, v_sel, key_pos, qposc, validc, rel_bias)

    nflat = bt * n_chunks
    qpos_c = jnp.broadcast_to(q_pos.reshape(1, n_chunks, QUERY_CHUNK),
                              (bt, n_chunks, QUERY_CHUNK)).reshape(nflat, QUERY_CHUNK)
    b_c = jnp.repeat(jnp.arange(bt, dtype=jnp.int32), n_chunks)
    out = lax.map(chunk, (q.reshape(nflat, QUERY_CHUNK, nh, dh),
                          sel.reshape(nflat, QUERY_CHUNK, nh, N_SEL),
                          valid.reshape(nflat, QUERY_CHUNK, nh, N_SEL),
                          qpos_c, b_c))
    return out.reshape(bt, s, nh * dh)


def _gather_rows(pool, new_rows, pt_row, pos, h_idx):
    posc = jnp.clip(pos, 0, PAST_LEN - 1)
    old = pool[pt_row[posc // PAGE_SIZE], posc % PAGE_SIZE, h_idx]
    new = new_rows[jnp.clip(pos - PAST_LEN, 0, new_rows.shape[0] - 1), h_idx]
    return jnp.where((pos < PAST_LEN)[..., None], old, new)


def _moba_sample(q, k_new, v_new, pool_k, pool_v, page_table, rel_bias):
    db, ds, nh, dh = q.shape
    total = PAST_LEN + ds
    ppb = MOBA_BLOCK // PAGE_SIZE
    n_full = PAST_LEN // MOBA_BLOCK
    n_cand = (total - 1) // MOBA_BLOCK
    past = pool_k[page_table[:, :n_full * ppb]]
    means = jnp.mean(past.reshape(db, n_full, MOBA_BLOCK, nh, dh), axis=2, dtype=jnp.float32)
    if n_cand > n_full:
        n_tail = PAST_LEN - n_full * MOBA_BLOCK
        n_extra = n_cand - n_full
        tail_rows = pool_k[page_table[:, n_full * ppb:]].reshape(db, n_tail, nh, dh)
        tail = jnp.concatenate([tail_rows, k_new], axis=1)[:, :n_extra * MOBA_BLOCK]
        tail_means = jnp.mean(tail.reshape(db, n_extra, MOBA_BLOCK, nh, dh), axis=2,
                              dtype=jnp.float32)
        means = jnp.concatenate([means, tail_means], axis=1)
    q_pos = PAST_LEN + jnp.arange(ds, dtype=jnp.int32)
    sel, valid = _select_blocks(q, means, q_pos // MOBA_BLOCK)
    h_idx = jnp.arange(nh)[None, :, None, None]
    offs = jnp.arange(MOBA_BLOCK, dtype=jnp.int32)

    def per_seq(args):
        qb, selb, validb, ptb, knb, vnb = args
        key_pos = selb[..., None] * MOBA_BLOCK + offs
        k_sel = _gather_rows(pool_k, knb, ptb, key_pos, h_idx)
        v_sel = _gather_rows(pool_v, vnb, ptb, key_pos, h_idx)
        return _moba_attend(qb, k_sel, v_sel, key_pos, q_pos, validb, rel_bias)

    out = lax.map(per_seq, (q, sel, valid, page_table, k_new, v_new))
    return out.reshape(db, ds, nh * dh)


def _s5_mixer(u, x0_re, x0_im, lam_re, lam_im, log_dt, b_re, b_im, c_re, c_im, d_skip,
              w_glu, b_glu):
    bt, ln, _ = u.shape
    f32 = jnp.float32
    ug = u.astype(f32).reshape(bt, ln, N_SSM_GROUPS, SSM_GROUP)
    lam = lax.complex(lam_re.astype(f32), lam_im.astype(f32))
    dt = jnp.exp(log_dt.astype(f32))[:, None]
    lam_bar = jnp.exp(lam * dt)
    b_bar = ((lam_bar - 1.0) / lam)[..., None] * lax.complex(b_re.astype(f32), b_im.astype(f32))
    bu = jnp.einsum('gpc,blgc->blgp', b_bar, ug.astype(jnp.complex64))
    x0 = lax.complex(x0_re.astype(f32), x0_im.astype(f32))
    bu = bu.at[:, 0].add(lam_bar * x0)
    a = jnp.broadcast_to(lam_bar, bu.shape)

    def combine(e1, e2):
        a1, b1 = e1
        a2, b2 = e2
        return a1 * a2, a2 * b1 + b2

    _, xs = lax.associative_scan(combine, (a, bu), axis=1)
    cm = lax.complex(c_re.astype(f32), c_im.astype(f32))
    y = jnp.real(jnp.einsum('gcp,blgp->blgc', cm, xs)) + d_skip.astype(f32) * ug
    g = jax.nn.gelu(y.reshape(bt, ln, SSM_WIDTH)).astype(u.dtype)
    out = g * jax.nn.sigmoid(g @ w_glu + b_glu)
    last = xs[:, -1]
    return out, jnp.real(last), jnp.imag(last)


def _trunk_layer(x, c, attn_fn, x0_re, x0_im, lp):
    bt, ln, _ = x.shape
    mod = (jax.nn.silu(c) @ lp['w_mod'] + lp['b_mod']).reshape(bt, N_MOD, 1, D_MODEL)

    def pre(i, h):
        return _rms_norm(h, lp['g_pre'][i]) * (1.0 + mod[:, 3 * i + 1]) + mod[:, 3 * i]

    def post(i, f, w):
        return w * mod[:, 3 * i + 2] * _rms_norm(f, lp['g_post'][i])

    f1 = _swiglu(pre(0, x), lp['w_ffn_gate'][0], lp['w_ffn_up'][0], lp['w_ffn_down'][0])
    x = x + post(0, f1, 0.5)
    proj = pre(1, x) @ lp['w_in']
    q, k, v, u = jnp.split(proj, [ATT_WIDTH, 2 * ATT_WIDTH, 3 * ATT_WIDTH], axis=-1)
    q = q.reshape(bt, ln, N_ATT_HEADS, HEAD_DIM)
    k = k.reshape(bt, ln, N_ATT_HEADS, HEAD_DIM)
    v = v.reshape(bt, ln, N_ATT_HEADS, HEAD_DIM)
    att = attn_fn(q, k, v)
    ssm, s_re, s_im = _s5_mixer(u, x0_re, x0_im, lp['lam_re'], lp['lam_im'], lp['log_dt'],
                                lp['b_re'], lp['b_im'], lp['c_re'], lp['c_im'], lp['d'],
                                lp['w_glu'], lp['b_glu'])
    mix = jnp.concatenate([att, ssm], axis=-1) @ lp['w_out']
    x = x + post(1, mix, 1.0)
    f2 = _swiglu(pre(2, x), lp['w_ffn_gate'][1], lp['w_ffn_up'][1], lp['w_ffn_down'][1])
    x = x + post(2, f2, 0.5)
    return x, k, v, s_re, s_im


def setup_inputs(seed: int = 0) -> dict:
    key = jax.random.key(seed)
    ks = iter(jax.random.split(key, 40))
    f32 = jnp.float32

    def nrm(shape, s):
        return jax.random.normal(next(ks), shape, f32) * s

    n_pages = PAST_LEN // PAGE_SIZE
    n_pool = (DEC_BATCH * n_pages * 5) // 4
    G, P, C = N_SSM_GROUPS, SSM_STATE, SSM_GROUP
    mix_w = ATT_WIDTH + SSM_WIDTH
    inp = {}
    inp['x_prompt'] = nrm((BATCH, SEQ, D_MODEL), 1.0)
    inp['x_sample'] = nrm((DEC_BATCH, DEC_SEQ, D_MODEL), 1.0)
    inp['cache_k'] = nrm((DEPTH, n_pool, PAGE_SIZE, N_ATT_HEADS, HEAD_DIM), 1.0)
    inp['cache_v'] = nrm((DEPTH, n_pool, PAGE_SIZE, N_ATT_HEADS, HEAD_DIM), 1.0)
    inp['state_ssm_re'] = nrm((DEPTH, DEC_BATCH, G, P), 0.1)
    inp['state_ssm_im'] = nrm((DEPTH, DEC_BATCH, G, P), 0.1)
    inp['page_table'] = jax.random.permutation(next(ks), n_pool)[:DEC_BATCH * n_pages].reshape(
        DEC_BATCH, n_pages).astype(jnp.int32)
    inp['c_prompt'] = nrm((BATCH, D_MODEL), 1.0)
    inp['c_sample'] = nrm((DEC_BATCH, D_MODEL), 1.0)
    inp['rel_bias'] = nrm((NUM_BUCKETS, N_ATT_HEADS), 0.5)
    inp['w_mod'] = nrm((DEPTH, D_MODEL, N_MOD * D_MODEL), D_MODEL ** -0.5)
    inp['b_mod'] = nrm((DEPTH, N_MOD * D_MODEL), 0.01)
    inp['g_pre'] = 1.0 + nrm((DEPTH, 3, D_MODEL), 0.01)
    inp['g_post'] = 1.0 + nrm((DEPTH, 3, D_MODEL), 0.01)
    inp['w_ffn_gate'] = nrm((DEPTH, 2, D_MODEL, D_FF), D_MODEL ** -0.5)
    inp['w_ffn_up'] = nrm((DEPTH, 2, D_MODEL, D_FF), D_MODEL ** -0.5)
    inp['w_ffn_down'] = nrm((DEPTH, 2, D_FF, D_MODEL), D_FF ** -0.5)
    inp['w_in'] = nrm((DEPTH, D_MODEL, 3 * ATT_WIDTH + SSM_WIDTH), D_MODEL ** -0.5)
    inp['w_out'] = nrm((DEPTH, mix_w, D_MODEL), mix_w ** -0.5)
    inp['ssm_lam_re'] = -0.5 + nrm((DEPTH, G, P), 0.01)
    inp['ssm_lam_im'] = jnp.pi * jnp.arange(P, dtype=f32) + nrm((DEPTH, G, P), 0.01)
    inp['ssm_log_dt'] = jax.random.uniform(next(ks), (DEPTH, G), f32,
                                           math.log(1e-3), math.log(1e-1))
    inp['ssm_b_re'] = nrm((DEPTH, G, P, C), (2 * C) ** -0.5)
    inp['ssm_b_im'] = nrm((DEPTH, G, P, C), (2 * C) ** -0.5)
    inp['ssm_c_re'] = nrm((DEPTH, G, C, P), P ** -0.5)
    inp['ssm_c_im'] = nrm((DEPTH, G, C, P), P ** -0.5)
    inp['ssm_d'] = nrm((DEPTH, G, C), 1.0)
    inp['w_glu'] = nrm((DEPTH, SSM_WIDTH, SSM_WIDTH), SSM_WIDTH ** -0.5)
    inp['b_glu'] = nrm((DEPTH, SSM_WIDTH), 0.01)
    return inp


def reference(x_prompt, x_sample, cache_k, cache_v, state_ssm_re, state_ssm_im, page_table,
              c_prompt, c_sample, rel_bias, w_mod, b_mod, g_pre, g_post, w_ffn_gate,
              w_ffn_up, w_ffn_down, w_in, w_out, ssm_lam_re, ssm_lam_im, ssm_log_dt,
              ssm_b_re, ssm_b_im, ssm_c_re, ssm_c_im, ssm_d, w_glu, b_glu):
    yp, ys = x_prompt, x_sample
    kp_l, vp_l, srp_l, sip_l = [], [], [], []
    ks_l, vs_l, srs_l, sis_l = [], [], [], []
    z0 = jnp.zeros((x_prompt.shape[0], N_SSM_GROUPS, SSM_STATE), jnp.float32)
    for l in range(DEPTH):
        lp = {'w_mod': w_mod[l], 'b_mod': b_mod[l], 'g_pre': g_pre[l], 'g_post': g_post[l],
              'w_ffn_gate': w_ffn_gate[l], 'w_ffn_up': w_ffn_up[l],
              'w_ffn_down': w_ffn_down[l], 'w_in': w_in[l], 'w_out': w_out[l],
              'lam_re': ssm_lam_re[l], 'lam_im': ssm_lam_im[l], 'log_dt': ssm_log_dt[l],
              'b_re': ssm_b_re[l], 'b_im': ssm_b_im[l], 'c_re': ssm_c_re[l],
              'c_im': ssm_c_im[l], 'd': ssm_d[l], 'w_glu': w_glu[l], 'b_glu': b_glu[l]}
        pool_k, pool_v = cache_k[l], cache_v[l]

        def attn_prompt(q, k, v):
            return _moba_prompt(q, k, v, rel_bias)

        def attn_sample(q, k, v, pool_k=pool_k, pool_v=pool_v):
            return _moba_sample(q, k, v, pool_k, pool_v, page_table, rel_bias)

        yp, kp, vp, srp, sip = _trunk_layer(yp, c_prompt, attn_prompt, z0, z0, lp)
        ys, kss, vss, srs, sis = _trunk_layer(ys, c_sample, attn_sample,
                                             state_ssm_re[l], state_ssm_im[l], lp)
        kp_l.append(kp)
        vp_l.append(vp)
        srp_l.append(srp.astype(state_ssm_re.dtype))
        sip_l.append(sip.astype(state_ssm_im.dtype))
        ks_l.append(kss)
        vs_l.append(vss)
        srs_l.append(srs.astype(state_ssm_re.dtype))
        sis_l.append(sis.astype(state_ssm_im.dtype))
    return (yp, ys, jnp.stack(kp_l), jnp.stack(vp_l), jnp.stack(srp_l), jnp.stack(sip_l),
            jnp.stack(ks_l), jnp.stack(vs_l), jnp.stack(srs_l), jnp.stack(sis_l))
```

```python
import functools
import math

import jax
import jax.numpy as jnp
from jax import lax
from jax.experimental import pallas as pl
from jax.experimental.pallas import tpu as pltpu

F32 = jnp.float32
BF16 = jnp.bfloat16
HIGHEST = lax.Precision.HIGHEST

D_MODEL = 1024
ATT_WIDTH = 512
SSM_WIDTH = 512
HEAD_DIM = 64
N_HEADS = 8
N_PAIRS = N_HEADS // 2
MOBA_BLOCK = 256
MOBA_TOPK = 3
PAGE_SIZE = 128
SSM_GROUP = 16
N_GROUPS = 32
SSM_STATE = 64
N_STATES = N_GROUPS * SSM_STATE
D_FF = 2816
NUM_BUCKETS = 32
MAX_EXACT = NUM_BUCKETS // 2
MAX_DISTANCE = 128
N_MOD = 9
RMS_EPS = 1e-6
NEG_INF = float("-inf")

LANES = 128
SUBLANES = 8
FF_CHUNK = 256
N_FF_CHUNKS = D_FF // FF_CHUNK
SSM_SLAB_GROUPS = 8
N_SLABS = N_GROUPS // SSM_SLAB_GROUPS
SLAB_CH = SSM_SLAB_GROUPS * SSM_GROUP
SLAB_STATES = SSM_SLAB_GROUPS * SSM_STATE


def _sigmoid(x):
    return 1.0 / (1.0 + jnp.exp(-x))


def _rms(x, g):
    ms = jnp.mean(x * x, axis=-1, keepdims=True)
    return x * lax.rsqrt(ms + RMS_EPS) * g


def _dot(a, b):
    return jnp.dot(a, b, preferred_element_type=F32)


def _dot_nt(a, b, precision=None):
    return lax.dot_general(a, b, (((1,), (1,)), ((), ())), precision=precision,
                           preferred_element_type=F32)


def _const_map(n):
    return lambda *_: (0,) * n


def _mod_kernel(c_ref, w_ref, b_ref, o_ref):
    c = c_ref[...]
    a = (c * _sigmoid(c)).astype(BF16)
    o_ref[...] = _dot(a, w_ref[...].astype(BF16)) + b_ref[...]


def _modulation(c_all, w_mod, b_mod):
    n = c_all.shape[0]
    return pl.pallas_call(
        _mod_kernel,
        grid=(N_MOD,),
        in_specs=[pl.BlockSpec((n, D_MODEL), lambda j: (0, 0)),
                  pl.BlockSpec((D_MODEL, D_MODEL), lambda j: (0, j)),
                  pl.BlockSpec((1, D_MODEL), lambda j: (0, j))],
        out_specs=pl.BlockSpec((n, D_MODEL), lambda j: (0, j)),
        out_shape=jax.ShapeDtypeStruct((n, N_MOD * D_MODEL), F32),
        name="modulation",
    )(c_all, w_mod, b_mod.reshape(1, -1))


def _bias_kernel(rb_ref, o_ref, *, off0, off_step, q_axis):
    h = pl.program_id(0)
    t = pl.program_id(1)
    shape = o_ref.shape
    r = lax.broadcasted_iota(jnp.int32, shape, 0)
    c = lax.broadcasted_iota(jnp.int32, shape, 1)
    qi, kj = (r, c) if q_axis == 0 else (c, r)
    dist = off0 + t * off_step + qi - kj
    n = jnp.maximum(dist, 0)
    nf = jnp.maximum(n, 1).astype(F32)
    large = MAX_EXACT + (jnp.log(nf / MAX_EXACT) / math.log(MAX_DISTANCE / MAX_EXACT)
                         * (NUM_BUCKETS - MAX_EXACT)).astype(jnp.int32)
    bucket = jnp.where(n < MAX_EXACT, n, jnp.minimum(large, NUM_BUCKETS - 1))
    out = jnp.zeros(shape, F32)
    for b in range(NUM_BUCKETS):
        out = jnp.where(bucket == b, rb_ref[b, h], out)
    o_ref[...] = jnp.where(dist >= 0, out, NEG_INF)


def _bias_tiles(rel_bias, n_t, rows, cols, off0, off_step, q_axis):
    return pl.pallas_call(
        functools.partial(_bias_kernel, off0=off0, off_step=off_step, q_axis=q_axis),
        grid=(N_HEADS, n_t),
        in_specs=[pl.BlockSpec(memory_space=pltpu.SMEM)],
        out_specs=pl.BlockSpec((None, None, rows, cols), lambda h, t: (h, t, 0, 0)),
        out_shape=jax.ShapeDtypeStruct((N_HEADS, n_t, rows, cols), F32),
        name="rel_bias_tiles",
    )(rel_bias)


def _ffn_kernel(x_ref, mod_ref, gpre_ref, gpost_ref, wg_ref, wu_ref, wd_ref, o_ref,
                h_ref, acc_ref):
    x = x_ref[...]
    h = _rms(x, gpre_ref[...]) * (1.0 + mod_ref[1]) + mod_ref[0]
    h_ref[...] = h.astype(BF16)
    acc_ref[...] = jnp.zeros_like(acc_ref)

    def chunk(j, carry):
        hb = h_ref[...]
        g = _dot(hb, wg_ref[j])
        u = _dot(hb, wu_ref[j])
        a = (g * _sigmoid(g) * u).astype(BF16)
        acc_ref[...] += _dot(a, wd_ref[j])
        return carry

    lax.fori_loop(0, N_FF_CHUNKS, chunk, 0)
    o_ref[...] = x + 0.5 * mod_ref[2] * _rms(acc_ref[...], gpost_ref[...])


def _ffn(x, mod3, g_pre, g_post, wg, wu, wd, tm):
    rows = x.shape[0]
    n_b, _, r, _ = mod3.shape
    tiles_per_b = rows // n_b // tm
    resident = dict(pipeline_mode=pl.Buffered(1))
    return pl.pallas_call(
        _ffn_kernel,
        grid=(rows // tm,),
        in_specs=[pl.BlockSpec((tm, D_MODEL), lambda i: (i, 0)),
                  pl.BlockSpec((None, 3, r, D_MODEL), lambda i: (i // tiles_per_b, 0, 0, 0)),
                  pl.BlockSpec((1, D_MODEL), _const_map(2)),
                  pl.BlockSpec((1, D_MODEL), _const_map(2)),
                  pl.BlockSpec((N_FF_CHUNKS, D_MODEL, FF_CHUNK), _const_map(3), **resident),
                  pl.BlockSpec((N_FF_CHUNKS, D_MODEL, FF_CHUNK), _const_map(3), **resident),
                  pl.BlockSpec((N_FF_CHUNKS, FF_CHUNK, D_MODEL), _const_map(3), **resident)],
        out_specs=pl.BlockSpec((tm, D_MODEL), lambda i: (i, 0)),
        out_shape=jax.ShapeDtypeStruct((rows, D_MODEL), F32),
        scratch_shapes=[pltpu.VMEM((tm, D_MODEL), BF16), pltpu.VMEM((tm, D_MODEL), F32)],
        compiler_params=pltpu.CompilerParams(dimension_semantics=("parallel",)),
        name="ffn_block",
    )(x, mod3, g_pre.reshape(1, -1), g_post.reshape(1, -1), wg, wu, wd)


def _ffn_weights(w_gate, w_up, w_down):
    def cols(w):
        return w.astype(BF16).reshape(D_MODEL, N_FF_CHUNKS, FF_CHUNK).transpose(1, 0, 2)
    return cols(w_gate), cols(w_up), w_down.astype(BF16).reshape(N_FF_CHUNKS, FF_CHUNK, D_MODEL)


def _inproj_kernel(x_ref, mod_ref, gpre_ref, w_ref, q_ref, k_ref, v_ref, u_ref):
    h = _rms(x_ref[...], gpre_ref[...]) * (1.0 + mod_ref[1]) + mod_ref[0]
    proj = _dot(h.astype(BF16), w_ref[...])
    q_ref[...] = proj[:, 0:ATT_WIDTH]
    k_ref[...] = proj[:, ATT_WIDTH:2 * ATT_WIDTH]
    v_ref[...] = proj[:, 2 * ATT_WIDTH:3 * ATT_WIDTH]
    u_ref[...] = proj[:, 3 * ATT_WIDTH:]


def _inproj(x, mod3, g_pre, w_in, tm, seq_len, time_major_u):
    rows = x.shape[0]
    n_b, _, r, _ = mod3.shape
    tiles_per_b = rows // n_b // tm
    row_spec = pl.BlockSpec((tm, ATT_WIDTH), lambda i: (i, 0))
    row_shape = jax.ShapeDtypeStruct((rows, ATT_WIDTH), F32)
    if time_major_u:
        tiles_per_seq = seq_len // tm
        u_spec = pl.BlockSpec((tm, SSM_WIDTH), lambda i: (i % tiles_per_seq, i // tiles_per_seq))
        u_shape = jax.ShapeDtypeStruct((seq_len, (rows // seq_len) * SSM_WIDTH), F32)
    else:
        u_spec, u_shape = row_spec, row_shape
    return pl.pallas_call(
        _inproj_kernel,
        grid=(rows // tm,),
        in_specs=[pl.BlockSpec((tm, D_MODEL), lambda i: (i, 0)),
                  pl.BlockSpec((None, 3, r, D_MODEL), lambda i: (i // tiles_per_b, 0, 0, 0)),
                  pl.BlockSpec((1, D_MODEL), _const_map(2)),
                  pl.BlockSpec((D_MODEL, 4 * ATT_WIDTH), _const_map(2),
                               pipeline_mode=pl.Buffered(1))],
        out_specs=[row_spec, row_spec, row_spec, u_spec],
        out_shape=[row_shape, row_shape, row_shape, u_shape],
        compiler_params=pltpu.CompilerParams(dimension_semantics=("parallel",)),
        name="in_projection",
    )(x, mod3, g_pre.reshape(1, -1), w_in)


def _gelu_tanh(x):
    return 0.5 * x * (1.0 + jnp.tanh(math.sqrt(2.0 / math.pi) * (x + 0.044715 * (x * x * x))))


def _outproj_kernel(x_ref, att_ref, y_ref, mod_ref, gpost_ref, wglu_ref, bglu_ref, wo_ref, o_ref):
    g = _gelu_tanh(y_ref[...])
    z = _dot(g.astype(BF16), wglu_ref[...]) + bglu_ref[...]
    ssm = g * _sigmoid(z)
    mix = (_dot(att_ref[...].astype(BF16), wo_ref[0:ATT_WIDTH, :])
           + _dot(ssm.astype(BF16), wo_ref[ATT_WIDTH:, :]))
    o_ref[...] = x_ref[...] + mod_ref[2] * _rms(mix, gpost_ref[...])


def _outproj(x, att, y, mod3, g_post, w_glu, b_glu, w_out, tm, seq_len, time_major_y):
    rows = x.shape[0]
    n_b, _, r, _ = mod3.shape
    tiles_per_b = rows // n_b // tm
    if time_major_y:
        tiles_per_seq = seq_len // tm
        y_spec = pl.BlockSpec((tm, SSM_WIDTH), lambda i: (i % tiles_per_seq, i // tiles_per_seq))
    else:
        y_spec = pl.BlockSpec((tm, SSM_WIDTH), lambda i: (i, 0))
    resident = dict(pipeline_mode=pl.Buffered(1))
    return pl.pallas_call(
        _outproj_kernel,
        grid=(rows // tm,),
        in_specs=[pl.BlockSpec((tm, D_MODEL), lambda i: (i, 0)),
                  pl.BlockSpec((tm, ATT_WIDTH), lambda i: (i, 0)),
                  y_spec,
                  pl.BlockSpec((None, 3, r, D_MODEL), lambda i: (i // tiles_per_b, 0, 0, 0)),
                  pl.BlockSpec((1, D_MODEL), _const_map(2)),
                  pl.BlockSpec((SSM_WIDTH, SSM_WIDTH), _const_map(2), **resident),
                  pl.BlockSpec((1, SSM_WIDTH), _const_map(2)),
                  pl.BlockSpec((D_MODEL, D_MODEL), _const_map(2), **resident)],
        out_specs=pl.BlockSpec((tm, D_MODEL), lambda i: (i, 0)),
        out_shape=jax.ShapeDtypeStruct((rows, D_MODEL), F32),
        compiler_params=pltpu.CompilerParams(dimension_semantics=("parallel",)),
        name="out_projection",
    )(x, att, y, mod3, g_post.reshape(1, -1), w_glu, b_glu.reshape(1, -1), w_out)


def _ssm_param_kernel(lr_ref, li_ref, ldt_ref, br_ref, bi_ref, lbr_ref, lbi_ref, bbr_ref, bbi_ref):
    lr, li = lr_ref[...], li_ref[...]
    dt = jnp.exp(ldt_ref[...])
    e = jnp.exp(lr * dt)
    lbr = e * jnp.cos(li * dt)
    lbi = e * jnp.sin(li * dt)
    lbr_ref[...] = lbr
    lbi_ref[...] = lbi
    nr, ni = lbr - 1.0, lbi
    den = lr * lr + li * li
    cr = (nr * lr + ni * li) / den
    ci = (ni * lr - nr * li) / den
    br, bi = br_ref[...], bi_ref[...]
    bbr_ref[...] = cr * br - ci * bi
    bbi_ref[...] = cr * bi + ci * br


def _ssm_params(lam_re, lam_im, log_dt, b_re, b_im):
    def per_channel(a):
        return jnp.repeat(a, SSM_GROUP, axis=0)

    def chan_rows(b):
        return b.transpose(0, 2, 1).reshape(N_GROUPS * SSM_GROUP, SSM_STATE)

    shape = jax.ShapeDtypeStruct((N_GROUPS * SSM_GROUP, SSM_STATE), F32)
    lbr, lbi, bbr, bbi = pl.pallas_call(
        _ssm_param_kernel,
        out_shape=[shape] * 4,
        name="ssm_discretise",
    )(per_channel(lam_re), per_channel(lam_im), per_channel(log_dt.reshape(-1, 1)),
      chan_rows(b_re), chan_rows(b_im))
    gcp = (N_GROUPS, SSM_GROUP, SSM_STATE)
    return lbr[::SSM_GROUP], lbi[::SSM_GROUP], bbr.reshape(gcp), bbi.reshape(gcp)


def _ssm_matrices(bbar_re, bbar_im, c_re, c_im):
    eye = jnp.eye(SSM_SLAB_GROUPS, dtype=F32)

    def in_mat(b):
        b = b.reshape(N_SLABS, SSM_SLAB_GROUPS, SSM_GROUP, SSM_STATE)
        return jnp.einsum("sgcp,gh->sgchp", b, eye).reshape(N_SLABS, SLAB_CH, SLAB_STATES)

    def out_mat(c):
        c = c.reshape(N_SLABS, SSM_SLAB_GROUPS, SSM_GROUP, SSM_STATE)
        return jnp.einsum("sgcp,gh->shpgc", c, eye).reshape(N_SLABS, SLAB_STATES, SLAB_CH)

    b_mat = jnp.concatenate([in_mat(bbar_re), in_mat(bbar_im)], axis=2).astype(BF16)
    c_mat = jnp.concatenate([out_mat(c_re), -out_mat(c_im)], axis=1).astype(BF16)
    return b_mat, c_mat


def _ssm_kernel(u_ref, bmat_ref, cmat_ref, lam_ref, x0_ref, d_ref, y_ref, xl_ref,
                xs_ref, st_ref, *, n_b, t_chunk):
    c = pl.program_id(1)

    @pl.when(c == 0)
    def _():
        st_ref[:, 0:SLAB_STATES] = x0_ref[0]
        st_ref[:, SLAB_STATES:] = x0_ref[1]

    u = u_ref[...]
    xs_ref[...] = _dot(u.astype(BF16), bmat_ref[...])
    lr = jnp.broadcast_to(lam_ref[0:1, :], (n_b, SLAB_STATES))
    li = jnp.broadcast_to(lam_ref[1:2, :], (n_b, SLAB_STATES))

    def step(t, carry):
        xr, xi = carry
        rows = pl.ds(pl.multiple_of(t * n_b, n_b), n_b)
        nr = lr * xr - li * xi + xs_ref[rows, 0:SLAB_STATES]
        ni = lr * xi + li * xr + xs_ref[rows, SLAB_STATES:]
        xs_ref[rows, 0:SLAB_STATES] = nr
        xs_ref[rows, SLAB_STATES:] = ni
        return nr, ni

    xr, xi = lax.fori_loop(0, t_chunk, step, (st_ref[:, 0:SLAB_STATES], st_ref[:, SLAB_STATES:]),
                           unroll=min(t_chunk, 8))
    st_ref[:, 0:SLAB_STATES] = xr
    st_ref[:, SLAB_STATES:] = xi
    y_ref[...] = _dot(xs_ref[...].astype(BF16), cmat_ref[...]) + d_ref[...] * u

    @pl.when(c == pl.num_programs(1) - 1)
    def _():
        xl_ref[0] = xr
        xl_ref[1] = xi


def _ssm_scan(u_tm, b_mat, c_mat, lam_bar, x0, d_row, n_b, t_chunk):
    rows = u_tm.shape[0]
    chunk_rows = n_b * t_chunk
    return pl.pallas_call(
        functools.partial(_ssm_kernel, n_b=n_b, t_chunk=t_chunk),
        grid=(N_SLABS, rows // chunk_rows),
        in_specs=[pl.BlockSpec((chunk_rows, SLAB_CH), lambda s, c: (c, s)),
                  pl.BlockSpec((None, SLAB_CH, 2 * SLAB_STATES), lambda s, c: (s, 0, 0)),
                  pl.BlockSpec((None, 2 * SLAB_STATES, SLAB_CH), lambda s, c: (s, 0, 0)),
                  pl.BlockSpec((2, SLAB_STATES), lambda s, c: (0, s)),
                  pl.BlockSpec((2, n_b, SLAB_STATES), lambda s, c: (0, 0, s)),
                  pl.BlockSpec((1, SLAB_CH), lambda s, c: (0, s))],
        out_specs=[pl.BlockSpec((chunk_rows, SLAB_CH), lambda s, c: (c, s)),
                   pl.BlockSpec((2, n_b, SLAB_STATES), lambda s, c: (0, 0, s))],
        out_shape=[jax.ShapeDtypeStruct((rows, SSM_WIDTH), F32),
                   jax.ShapeDtypeStruct((2, n_b, N_STATES), F32)],
        scratch_shapes=[pltpu.VMEM((chunk_rows, 2 * SLAB_STATES), F32),
                        pltpu.VMEM((n_b, 2 * SLAB_STATES), F32)],
        compiler_params=pltpu.CompilerParams(dimension_semantics=("parallel", "arbitrary")),
        name="s5_scan",
    )(u_tm, b_mat, c_mat, lam_bar, x0, d_row)


def _moba_prompt_kernel(c31_ref, q_ref, k_ref, v_ref, bias_ref, o_ref,
                        kb_ref, vt_ref, means_ref, qh_ref, sel_ref, m_ref, l_ref, acc_ref,
                        *, n_blk):
    p = pl.program_id(1)
    qi = pl.program_id(2)

    @pl.when(qi == 0)
    def _():
        for n in range(n_blk):
            kblk = k_ref[n * MOBA_BLOCK:(n + 1) * MOBA_BLOCK, :]
            kb_ref[n] = kblk.astype(BF16)
            means_ref[n:n + 1, :] = jnp.sum(kblk, axis=0, keepdims=True) * (1.0 / MOBA_BLOCK)
            vt_ref[n] = v_ref[n * MOBA_BLOCK:(n + 1) * MOBA_BLOCK, :].T.astype(BF16)

    q_t = q_ref[...].T
    dim_row = lax.broadcasted_iota(jnp.int32, q_t.shape, 0)
    blk_row = lax.broadcasted_iota(jnp.int32, (n_blk, MOBA_BLOCK), 0)
    mean_lane = lax.broadcasted_iota(jnp.int32, (n_blk, LANES), 1)
    means = means_ref[...]
    for h in range(2):
        in_head = (dim_row >= h * HEAD_DIM) & (dim_row < (h + 1) * HEAD_DIM)
        qh_ref[h] = (jnp.where(in_head, q_t, 0.0) * (HEAD_DIM ** -0.5)).astype(BF16)
        mean_h = jnp.where((mean_lane >= h * HEAD_DIM) & (mean_lane < (h + 1) * HEAD_DIM), means, 0.0)
        sc = jnp.dot(mean_h, q_t, precision=HIGHEST, preferred_element_type=F32)
        cand = blk_row < qi
        sc = jnp.where(cand, sc, NEG_INF)
        rank = jnp.zeros(sc.shape, F32)
        for m in range(n_blk - 1):
            sm = sc[m:m + 1, :]
            beats = (sm > sc) | ((sm == sc) & (blk_row > m))
            rank = rank + jnp.where(beats, 1.0, 0.0)
        sel_ref[h] = jnp.where(cand & (rank < MOBA_TOPK), 1.0, 0.0)
        m_ref[h] = jnp.full(m_ref.shape[1:], NEG_INF, F32)
        l_ref[h] = jnp.zeros(l_ref.shape[1:], F32)
        acc_ref[h] = jnp.zeros(acc_ref.shape[1:], F32)

    def tile(kj, bias_of_head, use_sel):
        kblk = kb_ref[kj]
        for h in range(2):
            s = _dot(kblk, qh_ref[h]) + bias_of_head(h)
            if use_sel:
                s = jnp.where(sel_ref[h, pl.ds(kj, 1), :] > 0.5, s, NEG_INF)
            m_old = m_ref[h]
            m_new = jnp.maximum(m_old, jnp.max(s, axis=0, keepdims=True))
            alpha = jnp.exp(m_old - m_new)
            pr = jnp.exp(s - m_new)
            l_ref[h] = alpha * l_ref[h] + jnp.sum(pr, axis=0, keepdims=True)
            vt = vt_ref[kj, h * HEAD_DIM:(h + 1) * HEAD_DIM, :]
            acc_ref[h] = alpha * acc_ref[h] + _dot(vt, pr.astype(BF16))
            m_ref[h] = m_new

    tile(qi, lambda h: bias_ref[h, 0], False)

    @pl.when(qi >= 1)
    def _():
        tile(qi - 1, lambda h: bias_ref[h, 1], True)

    def far(kj, carry):
        tile(kj, lambda h: c31_ref[2 * p + h], True)
        return carry

    lax.fori_loop(0, jnp.maximum(qi - 1, 0), far, 0)

    out_t = jnp.concatenate([acc_ref[h] / l_ref[h] for h in range(2)], axis=0)
    o_ref[...] = out_t.T


def _moba_prompt(q, k, v, bias_t, c31, n_batch, seq_len):
    n_blk = seq_len // MOBA_BLOCK
    slab = pl.BlockSpec((seq_len, LANES), lambda b, p, qi: (b, p))
    qo_spec = pl.BlockSpec((MOBA_BLOCK, LANES), lambda b, p, qi: (b * n_blk + qi, p))
    return pl.pallas_call(
        functools.partial(_moba_prompt_kernel, n_blk=n_blk),
        grid=(n_batch, N_PAIRS, n_blk),
        in_specs=[pl.BlockSpec(memory_space=pltpu.SMEM),
                  qo_spec, slab, slab,
                  pl.BlockSpec((2, 2, MOBA_BLOCK, MOBA_BLOCK), lambda b, p, qi: (p, 0, 0, 0))],
        out_specs=qo_spec,
        out_shape=jax.ShapeDtypeStruct(q.shape, F32),
        scratch_shapes=[pltpu.VMEM((n_blk, MOBA_BLOCK, LANES), BF16),
                        pltpu.VMEM((n_blk, LANES, MOBA_BLOCK), BF16),
                        pltpu.VMEM((n_blk, LANES), F32),
                        pltpu.VMEM((2, LANES, MOBA_BLOCK), BF16),
                        pltpu.VMEM((2, n_blk, MOBA_BLOCK), F32),
                        pltpu.VMEM((2, 1, MOBA_BLOCK), F32),
                        pltpu.VMEM((2, 1, MOBA_BLOCK), F32),
                        pltpu.VMEM((2, HEAD_DIM, MOBA_BLOCK), F32)],
        compiler_params=pltpu.CompilerParams(
            dimension_semantics=("parallel", "parallel", "arbitrary")),
        name="moba_prompt",
    )(c31, q, k, v, bias_t)


def _moba_sample_kernel(pt_ref, q_ref, kn_ref, vn_ref, ka_ref, kb_ref, va_ref, vb_ref,
                        blast_ref, bown_ref, c31_ref, o_ref,
                        qbd_ref, qf_ref, acc_ref, m_ref, l_ref, means_ref, *, n_blk, n_tok):
    n = pl.program_id(1)
    n_rows = N_HEADS * n_tok

    @pl.when(n == 0)
    def _():
        q_rep = jnp.concatenate([q_ref[...]] * N_HEADS, axis=0)
        row_head = lax.broadcasted_iota(jnp.int32, q_rep.shape, 0) // n_tok
        lane_head = lax.broadcasted_iota(jnp.int32, q_rep.shape, 1) // HEAD_DIM
        q_bd = jnp.where(row_head == lane_head, q_rep, 0.0)
        qf_ref[...] = q_bd
        qbd_ref[...] = (q_bd * (HEAD_DIM ** -0.5)).astype(BF16)
        means_ref[...] = jnp.zeros_like(means_ref)
        m_ref[...] = jnp.full(m_ref.shape, NEG_INF, F32)
        l_ref[...] = jnp.zeros_like(l_ref)

    kblk = jnp.concatenate([ka_ref[...], kb_ref[...]], axis=0)
    vblk = jnp.concatenate([va_ref[...], vb_ref[...]], axis=0)
    means_ref[pl.ds(n, 1), :] = jnp.sum(kblk, axis=0, keepdims=True) * (1.0 / MOBA_BLOCK)
    s = _dot_nt(qbd_ref[...], kblk.astype(BF16))
    s = s + jnp.where(n == n_blk - 1, blast_ref[...], c31_ref[...])
    m_n = jnp.max(s, axis=-1, keepdims=True)
    pr = jnp.exp(s - m_n)
    l_n = jnp.sum(pr, axis=-1, keepdims=True)
    acc_ref[n] = _dot(pr.astype(BF16), vblk.astype(BF16))
    blk_lane = lax.broadcasted_iota(jnp.int32, m_ref.shape, 1)
    m_ref[...] = jnp.where(blk_lane == n, m_n, m_ref[...])
    l_ref[...] = jnp.where(blk_lane == n, l_n, l_ref[...])

    @pl.when(n == n_blk - 1)
    def _():
        sc = _dot_nt(qf_ref[...], means_ref[...], precision=HIGHEST)
        sc = jnp.where(blk_lane < n_blk, sc, NEG_INF)
        sel = jnp.zeros(sc.shape, jnp.bool_)
        for _ in range(MOBA_TOPK):
            mx = jnp.max(sc, axis=-1, keepdims=True)
            first = jnp.min(jnp.where(sc == mx, blk_lane, LANES), axis=-1, keepdims=True)
            pick = (blk_lane == first) & (blk_lane < n_blk)
            sel = sel | pick
            sc = jnp.where(pick, NEG_INF, sc)
        zpad = jnp.zeros((LANES - n_tok, ATT_WIDTH), F32)
        k_own = jnp.concatenate([kn_ref[...], zpad], axis=0).astype(BF16)
        v_own = jnp.concatenate([vn_ref[...], zpad], axis=0).astype(BF16)
        s_own = _dot_nt(qbd_ref[...], k_own) + bown_ref[...]
        m_all = m_ref[...]
        m_tot = jnp.maximum(jnp.max(jnp.where(sel, m_all, NEG_INF), axis=-1, keepdims=True),
                            jnp.max(s_own, axis=-1, keepdims=True))
        w = jnp.where(sel, jnp.exp(m_all - m_tot), 0.0)
        p_own = jnp.exp(s_own - m_tot)
        l_tot = (jnp.sum(w * l_ref[...], axis=-1, keepdims=True)
                 + jnp.sum(p_own, axis=-1, keepdims=True))
        out = _dot(p_own.astype(BF16), v_own)
        for j in range(n_blk):
            out = out + w[:, j:j + 1] * acc_ref[j]
        out = out / l_tot
        lane_head = lax.broadcasted_iota(jnp.int32, (n_tok, ATT_WIDTH), 1) // HEAD_DIM
        att = jnp.zeros((n_tok, ATT_WIDTH), F32)
        for h in range(N_HEADS):
            att = jnp.where(lane_head == h, out[h * n_tok:(h + 1) * n_tok, :], att)
        o_ref[...] = att


def _moba_sample(q, k_new, v_new, pool_k, pool_v, page_table, bias_last, bias_own, c31_rows, n_tok):
    n_seq, n_pages = page_table.shape
    pages_per_blk = MOBA_BLOCK // PAGE_SIZE
    n_blk = n_pages // pages_per_blk
    n_rows = N_HEADS * n_tok
    tok_spec = pl.BlockSpec((n_tok, ATT_WIDTH), lambda b, n, pt: (b, 0))

    def page_spec(j):
        return pl.BlockSpec((None, PAGE_SIZE, ATT_WIDTH),
                            lambda b, n, pt: (pt[b, pages_per_blk * n + j], 0, 0))

    grid_spec = pltpu.PrefetchScalarGridSpec(
        num_scalar_prefetch=1,
        grid=(n_seq, n_blk),
        in_specs=[tok_spec, tok_spec, tok_spec,
                  page_spec(0), page_spec(1), page_spec(0), page_spec(1),
                  pl.BlockSpec((n_rows, MOBA_BLOCK), lambda b, n, pt: (0, 0)),
                  pl.BlockSpec((n_rows, LANES), lambda b, n, pt: (0, 0)),
                  pl.BlockSpec((n_rows, 1), lambda b, n, pt: (0, 0))],
        out_specs=tok_spec,
        scratch_shapes=[pltpu.VMEM((n_rows, ATT_WIDTH), BF16),
                        pltpu.VMEM((n_rows, ATT_WIDTH), F32),
                        pltpu.VMEM((n_blk, n_rows, ATT_WIDTH), F32),
                        pltpu.VMEM((n_rows, LANES), F32),
                        pltpu.VMEM((n_rows, LANES), F32),
                        pltpu.VMEM((LANES, ATT_WIDTH), F32)])
    return pl.pallas_call(
        functools.partial(_moba_sample_kernel, n_blk=n_blk, n_tok=n_tok),
        grid_spec=grid_spec,
        out_shape=jax.ShapeDtypeStruct(q.shape, F32),
        compiler_params=pltpu.CompilerParams(dimension_semantics=("parallel", "arbitrary")),
        name="moba_sample",
    )(page_table, q, k_new, v_new, pool_k, pool_k, pool_v, pool_v, bias_last, bias_own, c31_rows)


def _mod_per_batch(mod, i):
    return mod.reshape(mod.shape[0], 3, 3, 1, D_MODEL)[:, i]


def _mod_per_row(mod, i, n_tok):
    m = mod.reshape(mod.shape[0], 3, 3, D_MODEL)[:, i]
    m = jnp.broadcast_to(m[:, None], (m.shape[0], n_tok, 3, D_MODEL))
    return m.reshape(-1, 3, D_MODEL).transpose(1, 0, 2)[None]


def kernel(x_prompt, x_sample, cache_k, cache_v, state_ssm_re, state_ssm_im, page_table, c_prompt, c_sample, rel_bias, w_mod, b_mod, g_pre, g_post, w_ffn_gate, w_ffn_up, w_ffn_down, w_in, w_out, ssm_lam_re, ssm_lam_im, ssm_log_dt, ssm_b_re, ssm_b_im, ssm_c_re, ssm_c_im, ssm_d, w_glu, b_glu):
    n_b, seq_len, _ = x_prompt.shape
    n_seq, n_tok, _ = x_sample.shape
    n_pool = cache_k.shape[1]
    assert cache_k.shape[0] == 1, "single-layer trunk"
    assert seq_len % MOBA_BLOCK == 0 and page_table.shape[1] % (MOBA_BLOCK // PAGE_SIZE) == 0
    assert page_table.shape[1] // (MOBA_BLOCK // PAGE_SIZE) <= LANES

    mod = _modulation(jnp.concatenate([c_prompt, c_sample], axis=0), w_mod[0], b_mod[0])
    mod_p, mod_s = mod[:n_b], mod[n_b:]
    ffn_w = [_ffn_weights(w_ffn_gate[0, i], w_ffn_up[0, i], w_ffn_down[0, i]) for i in range(2)]
    w_in_b = w_in[0].astype(BF16)
    w_out_b = w_out[0].astype(BF16)
    w_glu_b = w_glu[0].astype(BF16)
    lbr, lbi, bbr, bbi = _ssm_params(ssm_lam_re[0], ssm_lam_im[0], ssm_log_dt[0],
                                     ssm_b_re[0], ssm_b_im[0])
    b_mat, c_mat = _ssm_matrices(bbr, bbi, ssm_c_re[0], ssm_c_im[0])
    lam_bar = jnp.stack([lbr.reshape(-1), lbi.reshape(-1)])
    d_row = ssm_d[0].reshape(1, -1)
    gp, gq = g_pre[0], g_post[0]
    c31 = rel_bias[NUM_BUCKETS - 1]

    tm = 512
    rows = n_b * seq_len
    x = x_prompt.reshape(rows, D_MODEL)
    x = _ffn(x, _mod_per_batch(mod_p, 0), gp[0], gq[0], *ffn_w[0], tm=tm)
    mod1 = _mod_per_batch(mod_p, 1)
    q, k, v, u_tm = _inproj(x, mod1, gp[1], w_in_b, tm, seq_len, True)
    bias_t = _bias_tiles(rel_bias, 2, MOBA_BLOCK, MOBA_BLOCK, 0, MOBA_BLOCK, 1)
    att = _moba_prompt(q, k, v, bias_t, c31, n_b, seq_len)
    zeros0 = jnp.zeros((2, n_b, N_STATES), F32)
    y_tm, x_last = _ssm_scan(u_tm.reshape(seq_len * n_b, SSM_WIDTH), b_mat, c_mat, lam_bar,
                             zeros0, d_row, n_b, 64)
    x = _outproj(x, att, y_tm.reshape(seq_len, n_b * SSM_WIDTH), mod1, gq[1], w_glu_b, b_glu[0],
                 w_out_b, tm, seq_len, True)
    x = _ffn(x, _mod_per_batch(mod_p, 2), gp[2], gq[2], *ffn_w[1], tm=tm)
    y_prompt = x.reshape(n_b, seq_len, D_MODEL)
    k_prompt = k.reshape(1, n_b, seq_len, N_HEADS, HEAD_DIM)
    v_prompt = v.reshape(1, n_b, seq_len, N_HEADS, HEAD_DIM)
    sre_prompt = x_last[0].reshape(1, n_b, N_GROUPS, SSM_STATE)
    sim_prompt = x_last[1].reshape(1, n_b, N_GROUPS, SSM_STATE)

    rows_s = n_seq * n_tok
    xs = x_sample.reshape(rows_s, D_MODEL)
    xs = _ffn(xs, _mod_per_row(mod_s, 0, n_tok), gp[0], gq[0], *ffn_w[0], tm=rows_s)
    mod1s = _mod_per_row(mod_s, 1, n_tok)
    qs, ks, vs, us = _inproj(xs, mod1s, gp[1], w_in_b, rows_s, n_tok, False)
    bias_last = _bias_tiles(rel_bias, 1, n_tok, MOBA_BLOCK, MOBA_BLOCK, 0, 0)
    bias_own = _bias_tiles(rel_bias, 1, n_tok, LANES, 0, 0, 0)
    pool_k = cache_k[0].reshape(n_pool, PAGE_SIZE, ATT_WIDTH)
    pool_v = cache_v[0].reshape(n_pool, PAGE_SIZE, ATT_WIDTH)
    att_s = _moba_sample(qs, ks, vs, pool_k, pool_v, page_table,
                         bias_last.reshape(N_HEADS * n_tok, MOBA_BLOCK),
                         bias_own.reshape(N_HEADS * n_tok, LANES),
                         jnp.repeat(c31, n_tok).reshape(-1, 1), n_tok)
    us_tm = us.reshape(n_seq, n_tok, SSM_WIDTH).transpose(1, 0, 2).reshape(rows_s, SSM_WIDTH)
    x0 = jnp.stack([state_ssm_re[0].reshape(n_seq, N_STATES), state_ssm_im[0].reshape(n_seq, N_STATES)])
    ys_tm, xs_last = _ssm_scan(us_tm, b_mat, c_mat, lam_bar, x0, d_row, n_seq, n_tok)
    ys_bt = ys_tm.reshape(n_tok, n_seq, SSM_WIDTH).transpose(1, 0, 2).reshape(rows_s, SSM_WIDTH)
    xs = _outproj(xs, att_s, ys_bt, mod1s, gq[1], w_glu_b, b_glu[0], w_out_b, rows_s, n_tok, False)
    xs = _ffn(xs, _mod_per_row(mod_s, 2, n_tok), gp[2], gq[2], *ffn_w[1], tm=rows_s)
    y_sample = xs.reshape(n_seq, n_tok, D_MODEL)
    k_sample = ks.reshape(1, n_seq, n_tok, N_HEADS, HEAD_DIM)
    v_sample = vs.reshape(1, n_seq, n_tok, N_HEADS, HEAD_DIM)
    sre_sample = xs_last[0].reshape(1, n_seq, N_GROUPS, SSM_STATE)
    sim_sample = xs_last[1].reshape(1, n_seq, N_GROUPS, SSM_STATE)

    return (y_prompt, y_sample, k_prompt, v_prompt, sre_prompt, sim_prompt,
            k_sample, v_sample, sre_sample, sim_sample)
```
